```python
import math
import jax
import jax.numpy as jnp
from jax import lax
import numpy as np

D_MODEL = 1024
BATCH = 4
SEQ = 4096
DEPTH = 4

GRID_W = 64
CTX_LEN = 256
N_MOD = 9
MACARON = 0.5
ALPHA = (2 * DEPTH) ** 0.25
BETA = (8 * DEPTH) ** -0.25
LN_EPS = 1e-5
RMS_EPS = 1e-6
F_TINY = 1e-20
FFN_HIDDEN = 2816

HG_DK = 128
HG_DV = 128
HG_HEADS = (D_MODEL // 2) // HG_DV
HG_K = HG_HEADS * HG_DK
HG_V = HG_HEADS * HG_DV
HG_CHUNK = 64
POOL_WINDOWS = (2, 4, 8, 16)
POOL_GROUPS = 4
POOL_WIDTH = D_MODEL // 2
POOL_GC = POOL_WIDTH // POOL_GROUPS
EVEN_SIZES = (HG_K, HG_V, HG_V, HG_K, HG_K, POOL_WIDTH)
EVEN_IN = sum(EVEN_SIZES)
EVEN_SPLITS = tuple(sum(EVEN_SIZES[:i + 1]) for i in range(len(EVEN_SIZES) - 1))
EVEN_MIX = HG_V + POOL_WIDTH

DA_HD = 64
DA_VD = 2 * DA_HD
DA_HEADS = D_MODEL // DA_VD
DA_QK = DA_HEADS * 2 * DA_HD
ODD_IN = 2 * DA_QK + DA_HEADS * DA_VD
ODD_MIX = DA_HEADS * DA_VD
Q_BLOCK = 128
ROPE_BASE = 10000.0
ROPE_AXIS = DA_HD // 2

N_EVEN = (DEPTH + 1) // 2
N_ODD = DEPTH // 2

kernel_name = 'hybrid_hgrn2_pool_diffattn_macaron'


def layer_norm(x, g, b):
    xf = x.astype(jnp.float32)
    mu = jnp.mean(xf, axis=-1, keepdims=True)
    xc = xf - mu
    var = jnp.mean(xc * xc, axis=-1, keepdims=True)
    y = xc * lax.rsqrt(var + LN_EPS) * g.astype(jnp.float32) + b.astype(jnp.float32)
    return y.astype(x.dtype)


def rms_norm(x, w):
    xf = x.astype(jnp.float32)
    y = xf * lax.rsqrt(jnp.mean(xf * xf, axis=-1, keepdims=True) + RMS_EPS)
    return (y * w.astype(jnp.float32)).astype(x.dtype)


def modulate(x, shift, scale):
    return x * (1.0 + scale) + shift


def swiglu(h, w_in, w_out):
    a, b = jnp.split(h @ w_in, 2, axis=-1)
    return (jax.nn.silu(a) * b) @ w_out


def ffn_substep(x, mods, w_in, w_out, g, b):
    shift, scale, gate = mods
    y = swiglu(modulate(x, shift, scale), w_in, w_out)
    return layer_norm(ALPHA * x + MACARON * gate * y, g, b)


def to_heads(a, hd):
    bsz, n, _ = a.shape
    return a.reshape(bsz, n, -1, hd).transpose(0, 2, 1, 3)


def from_heads(a):
    bsz, nh, n, hd = a.shape
    return a.transpose(0, 2, 1, 3).reshape(bsz, n, nh * hd)


def gla_chunk_scan(q, k, v, log_f, s0):
    bsz, nh, t, _ = q.shape
    dv = v.shape[-1]
    n = t // HG_CHUNK

    def to_chunks(a):
        return jnp.moveaxis(a.reshape(bsz, nh, n, HG_CHUNK, a.shape[-1]), 2, 0)

    causal = jnp.tril(jnp.ones((HG_CHUNK, HG_CHUNK), dtype=bool))

    def step(s, inp):
        qi, ki, vi, gi = inp
        b = jnp.cumsum(gi.astype(jnp.float32), axis=-2)
        diff = b[..., :, None, :] - b[..., None, :, :]
        decay = jnp.where(causal[:, :, None], jnp.exp(jnp.minimum(diff, 0.0)), 0.0)
        scores = jnp.einsum('bhtd,bhsd,bhtsd->bhts', qi, ki, decay)
        o = (jnp.einsum('bhts,bhsv->bhtv', scores, vi)
             + jnp.einsum('bhtd,bhdv->bhtv', qi * jnp.exp(b), s))
        b_last = b[..., -1:, :]
        s_new = (jnp.exp(b_last[..., 0, :])[..., None] * s
                 + jnp.einsum('bhsd,bhsv->bhdv', ki * jnp.exp(b_last - b), vi))
        return s_new, o

    s_fin, o = lax.scan(step, s0, (to_chunks(q), to_chunks(k), to_chunks(v), to_chunks(log_f)))
    o = jnp.moveaxis(o, 0, 2).reshape(bsz, nh, t, dv)
    return o, s_fin


def hgrn2_gates(f_raw, lb):
    z = to_heads(f_raw, HG_DK).astype(jnp.float32)
    lbh = lb.reshape(HG_HEADS, 1, HG_DK)
    f = lbh + (1.0 - lbh) * jax.nn.sigmoid(z)
    log_f = jnp.log(jnp.maximum(f, F_TINY))
    return log_f, 1.0 - f


def hgrn2_bidir(q_c, i_c, f_c, q_l, i_l, f_l, lb):
    flip = lambda a: jnp.flip(a, axis=2)
    bsz = q_c.shape[0]
    zero = jnp.zeros((bsz, HG_HEADS, HG_DK, HG_DV), jnp.float32)
    lf, k = hgrn2_gates(f_c[0], lb[0])
    o_cf, s_cf = gla_chunk_scan(q_c, k, i_c, lf, zero)
    lf, k = hgrn2_gates(f_l[0], lb[0])
    o_lf, _ = gla_chunk_scan(q_l, k, i_l, lf, s_cf)
    lf, k = hgrn2_gates(f_c[1], lb[1])
    o_cb, s_cb = gla_chunk_scan(flip(q_c), flip(k), flip(i_c), flip(lf), zero)
    lf, k = hgrn2_gates(f_l[1], lb[1])
    o_lb, _ = gla_chunk_scan(flip(q_l), flip(k), flip(i_l), flip(lf), s_cb)
    return o_cf + flip(o_cb), o_lf + flip(o_lb)


def multiscale_pool(u, pool_w, pool_scale):
    bsz, n, _ = u.shape
    uf = u.astype(jnp.float32)
    csum = jnp.concatenate([jnp.zeros((bsz, 1, POOL_WIDTH), jnp.float32), jnp.cumsum(uf, axis=1)], axis=1)
    pos = jnp.arange(n)
    groups = []
    for gi, w in enumerate(POOL_WINDOWS):
        lo = jnp.clip(pos - w // 2, 0, n)
        hi = jnp.clip(pos + (w - w // 2), 0, n)
        sl = slice(gi * POOL_GC, (gi + 1) * POOL_GC)
        win_sum = csum[:, hi, sl] - csum[:, lo, sl]
        count = (hi - lo).astype(jnp.float32)[None, :, None]
        groups.append(win_sum / count - uf[:, :, sl])
    pooled = jnp.stack(groups, axis=2)
    y = jnp.einsum('bngc,gcd->bngd', pooled, pool_w.astype(jnp.float32)).reshape(bsz, n, POOL_WIDTH)
    return (y * pool_scale.astype(jnp.float32)).astype(u.dtype)


def hgrn2_pool_mixer(h_lat, h_ctx, w_in, w_out, lb, norm_w, pool_w, pool_scale, need_ctx):
    def split(h):
        q, i, g, f_fw, f_bw, u = jnp.split(h @ w_in, EVEN_SPLITS, axis=-1)
        return to_heads(q, HG_DK) * HG_DK ** -0.5, to_heads(i, HG_DV), g, (f_fw, f_bw), u

    q_l, i_l, g_l, f_l, u_l = split(h_lat)
    q_c, i_c, g_c, f_c, u_c = split(h_ctx)
    o_c, o_l = hgrn2_bidir(q_c, i_c, f_c, q_l, i_l, f_l, lb)

    def readout(o, g, u):
        rec = from_heads(rms_norm(o, norm_w)).astype(u.dtype) * jax.nn.silu(g)
        return jnp.concatenate([rec, multiscale_pool(u, pool_w, pool_scale)], axis=-1) @ w_out

    y_lat = readout(o_l, g_l, u_l)
    y_ctx = readout(o_c, g_c, u_c) if need_ctx else None
    return y_lat, y_ctx


def axial_rope_tables(t):
    rows = t // GRID_W
    row = jnp.repeat(jnp.arange(rows, dtype=jnp.float32), GRID_W)
    col = jnp.tile(jnp.arange(GRID_W, dtype=jnp.float32), rows)
    inv_freq = ROPE_BASE ** (-jnp.arange(0, ROPE_AXIS, 2, dtype=jnp.float32) / ROPE_AXIS)
    ang_r = row[:, None] * inv_freq
    ang_c = col[:, None] * inv_freq
    ang = jnp.concatenate([ang_r, ang_r, ang_c, ang_c], axis=-1)
    return jnp.cos(ang), jnp.sin(ang)


def rotate_half(a):
    a1, a2 = jnp.split(a, 2, axis=-1)
    return jnp.concatenate([-a2, a1], axis=-1)


def apply_axial_rope(a, cos, sin):
    a_r, a_c = jnp.split(a, 2, axis=-1)
    return a * cos + jnp.concatenate([rotate_half(a_r), rotate_half(a_c)], axis=-1) * sin


def lambda_init(layer):
    return 0.8 - 0.6 * math.exp(-0.3 * layer)


def diff_attention_mixer(h_lat, h_ctx, w_in, w_out, lam_vec, sub_w, lam_init, need_ctx):
    bsz, t, _ = h_lat.shape

    def project(h):
        n = h.shape[1]
        q, k, v = jnp.split(h @ w_in, (DA_QK, 2 * DA_QK), axis=-1)
        q = q.reshape(bsz, n, DA_HEADS, 2, DA_HD).transpose(0, 2, 3, 1, 4) * DA_HD ** -0.5
        k = k.reshape(bsz, n, DA_HEADS, 2, DA_HD).transpose(0, 2, 3, 1, 4)
        return q, k, to_heads(v, DA_VD)

    q_l, k_l, v_l = project(h_lat)
    q_c, k_c, v_c = project(h_ctx)
    cos, sin = axial_rope_tables(t)
    q_l = apply_axial_rope(q_l, cos, sin).astype(h_lat.dtype)
    k_l = apply_axial_rope(k_l, cos, sin).astype(h_lat.dtype)
    lv = lam_vec.astype(jnp.float32)
    lam = jnp.exp(jnp.sum(lv[0] * lv[1])) - jnp.exp(jnp.sum(lv[2] * lv[3])) + lam_init

    def diff_attend(q, k, v):
        s = jnp.einsum('bhmqd,bhmkd->bhmqk', q, k).astype(jnp.float32)
        p = jax.nn.softmax(s, axis=-1)
        w = p[:, :, 0] - lam * p[:, :, 1]
        return jnp.einsum('bhqk,bhkv->bhqv', w.astype(v.dtype), v)

    k_all = jnp.concatenate([k_c, k_l], axis=3)
    v_all = jnp.concatenate([v_c, v_l], axis=2)
    n_blk = t // Q_BLOCK
    q_blocks = jnp.moveaxis(q_l.reshape(bsz, DA_HEADS, 2, n_blk, Q_BLOCK, DA_HD), 3, 0)
    o_l = lax.map(lambda qb: diff_attend(qb, k_all, v_all), q_blocks)
    o_l = jnp.moveaxis(o_l, 0, 2).reshape(bsz, DA_HEADS, t, DA_VD)

    def readout(o):
        return from_heads(rms_norm(o, sub_w) * (1.0 - lam_init)) @ w_out

    y_lat = readout(o_l)
    y_ctx = readout(diff_attend(q_c, k_c, v_c)) if need_ctx else None
    return y_lat, y_ctx


def setup_inputs(seed: int = 0) -> dict:
    key = jax.random.key(seed)
    ks = jax.random.split(key, 20)
    D = D_MODEL

    def nrm(k, shape, s):
        return jax.random.normal(k, shape, jnp.float32) * s

    return {
        'x': nrm(ks[0], (BATCH, SEQ, D), 1.0),
        'c': nrm(ks[1], (BATCH, D), 1.0),
        'ctx': nrm(ks[2], (BATCH, CTX_LEN, D), 1.0),
        'c_ctx': nrm(ks[3], (D,), 1.0),
        'w_ada': nrm(ks[4], (DEPTH, D, N_MOD * D), 0.5 * D ** -0.5),
        'b_ada': nrm(ks[5], (DEPTH, N_MOD * D), 0.02),
        'ln_g': 1.0 + nrm(ks[6], (DEPTH, 3, D), 0.02),
        'ln_b': nrm(ks[7], (DEPTH, 3, D), 0.02),
        'w_ffn_in': nrm(ks[8], (DEPTH, 2, D, 2 * FFN_HIDDEN), D ** -0.5),
        'w_ffn_out': nrm(ks[9], (DEPTH, 2, FFN_HIDDEN, D), BETA * FFN_HIDDEN ** -0.5),
        'w_in_even': nrm(ks[10], (N_EVEN, D, EVEN_IN), D ** -0.5),
        'w_out_even': nrm(ks[11], (N_EVEN, EVEN_MIX, D), BETA * EVEN_MIX ** -0.5),
        'hg_lb': nrm(ks[12], (N_EVEN, 2, HG_K), 0.5),
        'hg_norm_w': 1.0 + nrm(ks[13], (N_EVEN, HG_DV), 0.02),
        'pool_w': nrm(ks[14], (N_EVEN, POOL_GROUPS, POOL_GC, POOL_GC), POOL_GC ** -0.5),
        'pool_scale': 1.0 + nrm(ks[15], (N_EVEN, POOL_WIDTH), 0.1),
        'w_in_odd': nrm(ks[16], (N_ODD, D, ODD_IN), D ** -0.5),
        'w_out_odd': nrm(ks[17], (N_ODD, ODD_MIX, D), BETA * ODD_MIX ** -0.5),
        'da_lambda': nrm(ks[18], (N_ODD, 4, DA_HD), 0.1),
        'da_sub_w': 1.0 + nrm(ks[19], (N_ODD, DA_VD), 0.02),
    }


def reference(x, c, ctx, c_ctx, w_ada, b_ada, ln_g, ln_b, w_ffn_in, w_ffn_out,
              w_in_even, w_out_even, hg_lb, hg_norm_w, pool_w, pool_scale,
              w_in_odd, w_out_odd, da_lambda, da_sub_w):
    lb_soft = jax.nn.softmax(hg_lb.astype(jnp.float32), axis=0)
    lb_all = jnp.cumsum(lb_soft, axis=0) - lb_soft[0]
    c_act = jax.nn.silu(c)
    cc_act = jax.nn.silu(c_ctx)
    for layer in range(DEPTH):
        last = layer == DEPTH - 1
        m_lat = jnp.split((c_act @ w_ada[layer] + b_ada[layer])[:, None, :], N_MOD, axis=-1)
        m_ctx = jnp.split(cc_act @ w_ada[layer] + b_ada[layer], N_MOD, axis=-1)
        x = ffn_substep(x, m_lat[0:3], w_ffn_in[layer, 0], w_ffn_out[layer, 0], ln_g[layer, 0], ln_b[layer, 0])
        ctx = ffn_substep(ctx, m_ctx[0:3], w_ffn_in[layer, 0], w_ffn_out[layer, 0], ln_g[layer, 0], ln_b[layer, 0])
        h_lat = modulate(x, m_lat[3], m_lat[4])
        h_ctx = modulate(ctx, m_ctx[3], m_ctx[4])
        if layer % 2 == 0:
            e = layer // 2
            y_lat, y_ctx = hgrn2_pool_mixer(h_lat, h_ctx, w_in_even[e], w_out_even[e], lb_all[e],
                                            hg_norm_w[e], pool_w[e], pool_scale[e], not last)
        else:
            o = layer // 2
            y_lat, y_ctx = diff_attention_mixer(h_lat, h_ctx, w_in_odd[o], w_out_odd[o], da_lambda[o],
                                                da_sub_w[o], lambda_init(layer), not last)
        x = layer_norm(ALPHA * x + m_lat[5] * y_lat, ln_g[layer, 1], ln_b[layer, 1])
        x = ffn_substep(x, m_lat[6:9], w_ffn_in[layer, 1], w_ffn_out[layer, 1], ln_g[layer, 2], ln_b[layer, 2])
        if not last:
            ctx = layer_norm(ALPHA * ctx + m_ctx[5] * y_ctx, ln_g[layer, 1], ln_b[layer, 1])
            ctx = ffn_substep(ctx, m_ctx[6:9], w_ffn_in[layer, 1], w_ffn_out[layer, 1], ln_g[layer, 2], ln_b[layer, 2])
    return x
```

```python
import functools
import math

import numpy as np
import jax
import jax.numpy as jnp
from jax import lax
from jax.experimental import pallas as pl
from jax.experimental.pallas import tpu as pltpu

F32 = jnp.float32
BF16 = jnp.bfloat16

N_MOD = 9
MACARON = 0.5
LN_EPS = 1e-5
RMS_EPS = 1e-6
F_TINY = 1e-20

HG_DK = 128
HG_DV = 128
POOL_WINDOWS = (2, 4, 8, 16)
POOL_HALO = 8
DA_HD = 64
DA_VD = 2 * DA_HD
GRID_W = 64
ROPE_BASE = 10000.0
ROPE_AXIS = DA_HD // 2

GLA_CHUNK = 128
GLA_LEVELS = 7
MOD_ROWS = 8
VMEM_LIMIT = 56 * 1024 * 1024


def _cparams(sem):
    return pltpu.CompilerParams(dimension_semantics=sem, vmem_limit_bytes=VMEM_LIMIT)


def _resident(shape):
    nd = len(shape)
    return pl.BlockSpec(shape, lambda *_: (0,) * nd, pipeline_mode=pl.Buffered(1))


def _silu(a):
    return a * jax.nn.sigmoid(a)


def _layer_norm(r, g, b):
    mu = jnp.mean(r, axis=-1, keepdims=True)
    rc = r - mu
    var = jnp.mean(rc * rc, axis=-1, keepdims=True)
    return rc * lax.rsqrt(var + LN_EPS) * g + b


def _dot(a, b):
    return jnp.dot(a, b, preferred_element_type=F32)


def _dot_nt(a, b):
    return lax.dot_general(a, b, (((1,), (1,)), ((), ())), preferred_element_type=F32)


def _dot_tn(a, b):
    return lax.dot_general(a, b, (((0,), (0,)), ((), ())), preferred_element_type=F32)


def _ada_kernel(c_ref, w_ref, b_ref, o_ref):
    a = _silu(c_ref[...]).astype(BF16)
    o_ref[0] = _dot(a, w_ref[0].astype(BF16)) + b_ref[0]


def _ada_call(cond, w_ada, b_ada):
    depth, d, n = w_ada.shape
    tn = 1024
    return pl.pallas_call(
        _ada_kernel,
        grid=(depth, n // tn),
        in_specs=[
            pl.BlockSpec((MOD_ROWS, d), lambda l, j: (0, 0)),
            pl.BlockSpec((1, d, tn), lambda l, j: (l, 0, j)),
            pl.BlockSpec((1, 1, tn), lambda l, j: (l, 0, j)),
        ],
        out_specs=pl.BlockSpec((1, MOD_ROWS, tn), lambda l, j: (l, 0, j)),
        out_shape=jax.ShapeDtypeStruct((depth, MOD_ROWS, n), F32),
        compiler_params=_cparams(("arbitrary", "arbitrary")),
        name="ada_mod",
    )(cond, w_ada, b_ada.reshape(depth, 1, n))


def _ffn_kernel(x_ref, m_ref, wa_ref, wb_ref, wo_ref, g_ref, b_ref, o_ref, h_ref, acc_ref,
                *, mi, li, alpha):
    x = x_ref[...]
    shift = m_ref[0, mi:mi + 1, :]
    scale = m_ref[0, mi + 1:mi + 2, :]
    gate = m_ref[0, mi + 2:mi + 3, :]
    h_ref[...] = (x * (1.0 + scale) + shift).astype(BF16)
    acc_ref[...] = jnp.zeros_like(acc_ref)

    def body(c, carry):
        h = h_ref[...]
        a = _dot(h, wa_ref[c])
        b = _dot(h, wb_ref[c])
        act = (_silu(a) * b).astype(BF16)
        acc_ref[...] += _dot(act, wo_ref[c])
        return carry

    lax.fori_loop(0, wa_ref.shape[0], body, 0)
    r = alpha * x + (MACARON * gate) * acc_ref[...]
    o_ref[...] = _layer_norm(r, g_ref[li:li + 1, :], b_ref[li:li + 1, :])


def _ffn_call(xs, mods_l, wa, wb, wo, ln_g_l, ln_b_l, *, mi, li, alpha, n_tiles, tm, tiles_per_seq, nb):
    r, d = xs.shape
    nch, _, th = wa.shape
    bid = lambda i: (jnp.minimum(i // tiles_per_seq, nb), 0, 0)
    return pl.pallas_call(
        functools.partial(_ffn_kernel, mi=mi, li=li, alpha=alpha),
        grid=(n_tiles,),
        in_specs=[
            pl.BlockSpec((tm, d), lambda i: (i, 0)),
            pl.BlockSpec((1, N_MOD, d), bid),
            _resident((nch, d, th)),
            _resident((nch, d, th)),
            _resident((nch, th, d)),
            _resident((3, d)),
            _resident((3, d)),
        ],
        out_specs=pl.BlockSpec((tm, d), lambda i: (i, 0)),
        out_shape=jax.ShapeDtypeStruct((n_tiles * tm, d), F32),
        scratch_shapes=[pltpu.VMEM((tm, d), BF16), pltpu.VMEM((tm, d), F32)],
        compiler_params=_cparams(("arbitrary",)),
        name="ffn_substep",
    )(xs, mods_l, wa, wb, wo, ln_g_l, ln_b_l)


def _inproj_kernel(*refs, q_cols, q_scale, rope_cols, cw):
    if rope_cols:
        x_ref, m_ref, w_ref, cos_ref, sa_ref, sb_ref, o_ref, h_ref = refs
    else:
        x_ref, m_ref, w_ref, o_ref, h_ref = refs
    x = x_ref[...]
    shift = m_ref[0, 3:4, :]
    scale = m_ref[0, 4:5, :]
    h_ref[...] = (x * (1.0 + scale) + shift).astype(BF16)
    n_out = w_ref.shape[1]
    for c in range(n_out // cw):
        z = _dot(h_ref[...], w_ref[:, c * cw:(c + 1) * cw])
        if c * cw < q_cols:
            z = z * q_scale
        for j in range(cw // 128):
            col = c * cw + j * 128
            a = z[:, j * 128:(j + 1) * 128]
            if col < rope_cols:
                a = (a * cos_ref[...] + pltpu.roll(a, 128 - 16, 1) * sb_ref[...]
                     + pltpu.roll(a, 16, 1) * sa_ref[...])
            o_ref[:, col:col + 128] = a.astype(o_ref.dtype)


def _inproj_call(xs, mods_l, w, *, q_cols, q_scale, rope, out_dtype, tm, tiles_per_seq, nb):
    r, d = xs.shape
    n_out = w.shape[1]
    n_tiles = r // tm
    bid = lambda i: (jnp.minimum(i // tiles_per_seq, nb), 0, 0)
    in_specs = [
        pl.BlockSpec((tm, d), lambda i: (i, 0)),
        pl.BlockSpec((1, N_MOD, d), bid),
        _resident((d, n_out)),
    ]
    args = [xs, mods_l, w]
    rope_cols = 0
    if rope is not None:
        n_lat_tiles = nb * tiles_per_seq
        tab = lambda i: (jnp.where(i < n_lat_tiles, i % tiles_per_seq, tiles_per_seq), 0)
        in_specs += [pl.BlockSpec((tm, 128), tab)] * 3
        args += list(rope)
        rope_cols = 2 * q_cols
    return pl.pallas_call(
        functools.partial(_inproj_kernel, q_cols=q_cols, q_scale=q_scale, rope_cols=rope_cols, cw=512),
        grid=(n_tiles,),
        in_specs=in_specs,
        out_specs=pl.BlockSpec((tm, n_out), lambda i: (i, 0)),
        out_shape=jax.ShapeDtypeStruct((r, n_out), out_dtype),
        scratch_shapes=[pltpu.VMEM((tm, d), BF16)],
        compiler_params=_cparams(("arbitrary",)),
        name="mixer_inproj",
    )(*args)


def _rope_tables(seq, tm):
    rows = seq // GRID_W
    row = np.repeat(np.arange(rows, dtype=np.float32), GRID_W)
    col = np.tile(np.arange(GRID_W, dtype=np.float32), rows)
    inv_freq = (ROPE_BASE ** (-np.arange(0, ROPE_AXIS, 2, dtype=np.float32) / ROPE_AXIS)).astype(np.float32)
    ang_r = row[:, None] * inv_freq
    ang_c = col[:, None] * inv_freq
    ang = np.concatenate([ang_r, ang_r, ang_c, ang_c], axis=-1).astype(np.float32)
    cos = np.cos(ang).astype(np.float32)
    sin = np.sin(ang).astype(np.float32)
    first = (np.arange(DA_HD) % ROPE_AXIS) < (ROPE_AXIS // 2)
    sin_a = np.where(first[None, :], 0.0, sin)
    sin_b = np.where(first[None, :], -sin, 0.0)

    def widen(t, pad_value):
        t = np.concatenate([t, t], axis=-1)
        pad = np.full((tm, 128), pad_value, np.float32)
        return jnp.asarray(np.concatenate([t, pad], axis=0), F32)

    return widen(cos, 1.0), widen(sin_a, 0.0), widen(sin_b, 0.0)


def _gla_constants(rev):
    c = GLA_CHUNK
    t = np.arange(c)[:, None]
    r = np.arange(c)[None, :]
    mats = [(r >= t) if rev else (r <= t)]
    for lv in range(1, GLA_LEVELS + 1):
        m = 1 << (lv - 1)
        mid = (t // (2 * m)) * (2 * m) + m
        mats.append((r >= mid) if rev else (r <= mid - 1))
    mbig = np.concatenate(mats, axis=0).astype(np.float32)
    x = np.arange(c)[:, None] ^ np.arange(c)[None, :]
    lev = np.where(x > 0, np.floor(np.log2(np.maximum(x, 1))).astype(np.int32) + 1, 0)
    valid = (r >= t) if rev else (r <= t)
    lev = np.where(valid, lev, -1).astype(np.int32)
    return jnp.asarray(mbig, BF16), jnp.asarray(lev, jnp.int32)


def _gla_kernel(q_ref, v_ref, f_ref, lb_ref, mbig_ref, lev_ref, o_ref, st_ref, br_ref, *, e, rev, heads):
    c = GLA_CHUNK

    @pl.when(pl.program_id(1) == 0)
    def _():
        st_ref[...] = jnp.zeros_like(st_ref)

    raw = lb_ref[...]
    ex = jnp.exp(raw - jnp.max(raw, axis=0, keepdims=True))
    soft = ex / jnp.sum(ex, axis=0, keepdims=True)
    lb = jnp.zeros_like(soft[0:1])
    for jj in range(1, e + 1):
        lb = lb + soft[jj:jj + 1]

    f = lb + (1.0 - lb) * jax.nn.sigmoid(f_ref[...])
    g = jnp.log(jnp.maximum(f, F_TINY))
    key = 1.0 - f
    g_hi = g.astype(BF16)
    g_lo = (g - g_hi.astype(F32)).astype(BF16)
    mb = mbig_ref[...]
    br_ref[...] = _dot(mb, g_hi) + _dot(mb, g_lo)
    lev = lev_ref[...]

    for h in range(heads):
        cs = slice(h * HG_DK, (h + 1) * HG_DK)
        b = br_ref[0:c, cs]
        q = q_ref[:, cs]
        k = key[:, cs]
        v = v_ref[:, cs].astype(BF16)
        a = jnp.where(lev == 0, _dot_nt(q.astype(BF16), k.astype(BF16)), 0.0)
        for lv in range(1, GLA_LEVELS + 1):
            ref_b = br_ref[lv * c:(lv + 1) * c, cs]
            qe = (q * jnp.exp(jnp.minimum(b - ref_b, 0.0))).astype(BF16)
            ke = (k * jnp.exp(jnp.minimum(ref_b - b, 0.0))).astype(BF16)
            a = jnp.where(lev == lv, _dot_nt(qe, ke), a)
        b_tot = b[0:1, :] if rev else b[c - 1:c, :]
        st = st_ref[h]
        qb = (q * jnp.exp(b)).astype(BF16)
        o_ref[:, cs] = _dot(a.astype(BF16), v) + _dot_nt(qb, st.astype(BF16))
        kb = (k * jnp.exp(b_tot - b)).astype(BF16)
        st_ref[h] = jnp.exp(b_tot) * st + _dot_tn(v, kb)


def _gla_call(z, lb_raw, consts, *, e, d, rev, nb, seq, ctx):
    r = z.shape[0]
    c = GLA_CHUNK
    heads = lb_raw.shape[-1] // HG_DK
    hk = heads * HG_DK
    nl, nc = seq // c, ctx // c
    lat_blocks = nb * nl

    def row_block(b, j):
        if rev:
            return jnp.where(j < nc, lat_blocks + b * nc + (nc - 1 - j), b * nl + (nl - 1 - (j - nc)))
        return jnp.where(j < nc, lat_blocks + b * nc + j, b * nl + (j - nc))

    f_col = 3 + d
    mbig, lev = consts
    return pl.pallas_call(
        functools.partial(_gla_kernel, e=e, rev=rev, heads=heads),
        grid=(nb, nl + nc),
        in_specs=[
            pl.BlockSpec((c, hk), lambda b, j: (row_block(b, j), 0)),
            pl.BlockSpec((c, hk), lambda b, j: (row_block(b, j), 1)),
            pl.BlockSpec((c, hk), lambda b, j: (row_block(b, j), f_col)),
            _resident(lb_raw.shape),
            _resident(mbig.shape),
            _resident(lev.shape),
        ],
        out_specs=pl.BlockSpec((c, hk), lambda b, j: (row_block(b, j), 0)),
        out_shape=jax.ShapeDtypeStruct((r, hk), F32),
        scratch_shapes=[pltpu.VMEM((heads, HG_DV, HG_DK), F32),
                        pltpu.VMEM(((GLA_LEVELS + 1) * c, hk), F32)],
        compiler_params=_cparams(("arbitrary", "arbitrary")),
        name="gla_bwd" if rev else "gla_fwd",
    )(z, z, z, lb_raw, mbig, lev)


def _even_readout_kernel(x_ref, m_ref, of_ref, ob_ref, g_ref, u_ref, up_ref, un_ref, nw_ref, pw_ref,
                         ps_ref, wo_ref, lg_ref, lb_ref, o_ref, ext_ref, mix_ref,
                         *, alpha, tm, seq, ctx, n_lat_tiles, heads):
    i = pl.program_id(0)
    is_lat = i < n_lat_tiles
    n = jnp.where(is_lat, seq, ctx)
    pos0 = jnp.where(is_lat, (i % (seq // tm)) * tm, ((i - n_lat_tiles) % (ctx // tm)) * tm)
    hv = heads * HG_DV

    o = of_ref[...] + ob_ref[...]
    gate_act = _silu(g_ref[...])
    for h in range(heads):
        cs = slice(h * HG_DV, (h + 1) * HG_DV)
        oh = o[:, cs]
        ms = jnp.mean(oh * oh, axis=-1, keepdims=True)
        mix_ref[:, cs] = (oh * lax.rsqrt(ms + RMS_EPS) * nw_ref[...] * gate_act[:, cs]).astype(BF16)

    u = u_ref[...]
    hp = POOL_HALO
    ext_ref[0:hp, :] = jnp.where(pos0 > 0, up_ref[...], 0.0)
    ext_ref[hp:hp + tm, :] = u
    ext_ref[hp + tm:2 * hp + tm, :] = jnp.where(pos0 + tm < n, un_ref[...], 0.0)
    pos = pos0 + lax.broadcasted_iota(jnp.int32, (tm, 1), 0)
    gc = u.shape[1] // len(POOL_WINDOWS)
    for gi, w in enumerate(POOL_WINDOWS):
        cs = slice(gi * gc, (gi + 1) * gc)
        hw = w // 2
        win = ext_ref[hp - hw:hp - hw + tm, cs]
        for jj in range(-hw + 1, w - hw):
            win = win + ext_ref[hp + jj:hp + jj + tm, cs]
        cnt = (jnp.minimum(pos + (w - hw), n) - jnp.maximum(pos - hw, 0)).astype(F32)
        pooled = win / cnt - u[:, cs]
        y = _dot(pooled.astype(BF16), pw_ref[gi].astype(BF16)) * ps_ref[:, cs]
        mix_ref[:, hv + gi * gc:hv + (gi + 1) * gc] = y.astype(BF16)

    y = _dot(mix_ref[...], wo_ref[...])
    r = alpha * x_ref[...] + m_ref[0, 5:6, :] * y
    o_ref[...] = _layer_norm(r, lg_ref[1:2, :], lb_ref[1:2, :])


def _even_readout_call(xs, mods_l, o_f, o_b, z, norm_w, pool_w, pool_scale, w_out, ln_g_l, ln_b_l,
                       *, alpha, nb, seq, ctx):
    r, d = xs.shape
    tm = 256
    assert ctx % tm == 0 and seq % tm == 0
    n_tiles = r // tm
    tiles_per_seq = seq // tm
    hv = o_f.shape[1]
    pw = z.shape[1] - 5 * hv
    heads = hv // HG_DV
    hb = tm // POOL_HALO
    last_hb = r // POOL_HALO - 1
    bid = lambda i: (jnp.minimum(i // tiles_per_seq, nb), 0, 0)
    u_col = 5 * hv // pw
    return pl.pallas_call(
        functools.partial(_even_readout_kernel, alpha=alpha, tm=tm, seq=seq, ctx=ctx,
                          n_lat_tiles=nb * tiles_per_seq, heads=heads),
        grid=(n_tiles,),
        in_specs=[
            pl.BlockSpec((tm, d), lambda i: (i, 0)),
            pl.BlockSpec((1, N_MOD, d), bid),
            pl.BlockSpec((tm, hv), lambda i: (i, 0)),
            pl.BlockSpec((tm, hv), lambda i: (i, 0)),
            pl.BlockSpec((tm, hv), lambda i: (i, 2)),
            pl.BlockSpec((tm, pw), lambda i: (i, u_col)),
            pl.BlockSpec((POOL_HALO, pw), lambda i: (jnp.maximum(i * hb - 1, 0), u_col)),
            pl.BlockSpec((POOL_HALO, pw), lambda i: (jnp.minimum((i + 1) * hb, last_hb), u_col)),
            _resident((1, HG_DV)),
            _resident(pool_w.shape),
            _resident((1, pw)),
            _resident(w_out.shape),
            _resident((3, d)),
            _resident((3, d)),
        ],
        out_specs=pl.BlockSpec((tm, d), lambda i: (i, 0)),
        out_shape=jax.ShapeDtypeStruct((r, d), F32),
        scratch_shapes=[pltpu.VMEM((tm + 2 * POOL_HALO, pw), F32), pltpu.VMEM((tm, hv + pw), BF16)],
        compiler_params=_cparams(("arbitrary",)),
        name="even_readout",
    )(xs, mods_l, o_f, o_b, z, z, z, z, norm_w.reshape(1, -1), pool_w, pool_scale.reshape(1, -1),
      w_out, ln_g_l, ln_b_l)


def _attn_kernel(*refs, lam_init, segs):
    lam_ref, q_ref = refs[0], refs[1]
    n_kv = len({s[0] for s in segs})
    k_refs = refs[2:2 + n_kv]
    v_refs = refs[2 + n_kv:2 + 2 * n_kv]
    o_ref, s1_ref, s2_ref = refs[-3], refs[-2], refs[-1]
    tq = q_ref.shape[0]

    lv = lam_ref[...]
    lam = (jnp.exp(jnp.sum(lv[0:1] * lv[1:2], axis=-1, keepdims=True))
           - jnp.exp(jnp.sum(lv[2:3] * lv[3:4], axis=-1, keepdims=True)) + lam_init)

    q = q_ref[...]
    lane = lax.broadcasted_iota(jnp.int32, (1, DA_VD), 1)
    zero = jnp.zeros_like(q)
    q1 = jnp.where(lane < DA_HD, q, zero)
    q2 = jnp.where(lane >= DA_HD, q, zero)

    m1 = jnp.full((tq, 1), -jnp.inf, F32)
    m2 = jnp.full((tq, 1), -jnp.inf, F32)
    col = 0
    for (src, off, n) in segs:
        kc = k_refs[src][off:off + n, :]
        s1 = _dot_nt(q1, kc)
        s2 = _dot_nt(q2, kc)
        s1_ref[:, col:col + n] = s1
        s2_ref[:, col:col + n] = s2
        m1 = jnp.maximum(m1, jnp.max(s1, axis=-1, keepdims=True))
        m2 = jnp.maximum(m2, jnp.max(s2, axis=-1, keepdims=True))
        col += n

    l1 = jnp.zeros((tq, 1), F32)
    l2 = jnp.zeros((tq, 1), F32)
    col = 0
    for (src, off, n) in segs:
        e1 = jnp.exp(s1_ref[:, col:col + n] - m1)
        e2 = jnp.exp(s2_ref[:, col:col + n] - m2)
        s1_ref[:, col:col + n] = e1
        s2_ref[:, col:col + n] = e2
        l1 = l1 + jnp.sum(e1, axis=-1, keepdims=True)
        l2 = l2 + jnp.sum(e2, axis=-1, keepdims=True)
        col += n

    a1 = 1.0 / l1
    a2 = lam / l2
    acc = jnp.zeros((tq, DA_VD), F32)
    col = 0
    for (src, off, n) in segs:
        w = (s1_ref[:, col:col + n] * a1 - s2_ref[:, col:col + n] * a2).astype(BF16)
        acc = acc + _dot(w, v_refs[src][off:off + n, :])
        col += n
    o_ref[...] = acc


def _attn_call(qkv, lam_vec, prev_out, *, lam_init, nb, seq, ctx, heads, ctx_queries):
    r = qkv.shape[0]
    qk = heads * DA_VD
    tq = 256
    tk = 512
    lat_ctx_blocks = nb * seq // ctx
    kcol, vcol = heads, 2 * heads
    ctx_k = pl.BlockSpec((ctx, DA_VD), lambda b, h, i: (lat_ctx_blocks + b, kcol + h))
    ctx_v = pl.BlockSpec((ctx, DA_VD), lambda b, h, i: (lat_ctx_blocks + b, vcol + h))
    if ctx_queries:
        nq = ctx // tq
        q_idx = lambda b, h, i: (nb * (seq // tq) + b * nq + i, h)
        segs = [(0, 0, ctx)]
        kv_specs = [ctx_k, ctx_v]
        kv_args = [qkv, qkv]
    else:
        nq = seq // tq
        q_idx = lambda b, h, i: (b * nq + i, h)
        segs = [(0, 0, ctx)] + [(1, o, tk) for o in range(0, seq, tk)]
        kv_specs = [ctx_k,
                    pl.BlockSpec((seq, DA_VD), lambda b, h, i: (b, kcol + h)),
                    ctx_v,
                    pl.BlockSpec((seq, DA_VD), lambda b, h, i: (b, vcol + h))]
        kv_args = [qkv, qkv, qkv, qkv]
    n_keys = sum(s[2] for s in segs)
    in_specs = [_resident(lam_vec.shape), pl.BlockSpec((tq, DA_VD), q_idx)] + kv_specs
    args = [lam_vec, qkv] + kv_args
    aliases = {}
    if prev_out is not None:
        in_specs.append(pl.BlockSpec(memory_space=pl.ANY))
        args.append(prev_out)
        aliases = {len(args) - 1: 0}
    kern = functools.partial(_attn_kernel, lam_init=lam_init, segs=tuple(segs))
    if prev_out is not None:
        inner = kern
        kern = lambda *refs: inner(*refs[:len(args) - 1], *refs[len(args):])
    return pl.pallas_call(
        kern,
        grid=(nb, heads, nq),
        in_specs=in_specs,
        out_specs=pl.BlockSpec((tq, DA_VD), q_idx),
        out_shape=jax.ShapeDtypeStruct((r, qk), F32),
        scratch_shapes=[pltpu.VMEM((tq, n_keys), F32), pltpu.VMEM((tq, n_keys), F32)],
        input_output_aliases=aliases,
        compiler_params=_cparams(("arbitrary", "arbitrary", "arbitrary")),
        name="diff_attn_ctx" if ctx_queries else "diff_attn_lat",
    )(*args)


def _odd_readout_kernel(x_ref, m_ref, a_ref, sw_ref, wo_ref, lg_ref, lb_ref, o_ref, mix_ref,
                        *, alpha, lam_init, heads):
    a = a_ref[...]
    for h in range(heads):
        cs = slice(h * DA_VD, (h + 1) * DA_VD)
        ah = a[:, cs]
        ms = jnp.mean(ah * ah, axis=-1, keepdims=True)
        mix_ref[:, cs] = (ah * lax.rsqrt(ms + RMS_EPS) * sw_ref[...] * (1.0 - lam_init)).astype(BF16)
    y = _dot(mix_ref[...], wo_ref[...])
    r = alpha * x_ref[...] + m_ref[0, 5:6, :] * y
    o_ref[...] = _layer_norm(r, lg_ref[1:2, :], lb_ref[1:2, :])


def _odd_readout_call(xs, mods_l, attn, sub_w, w_out, ln_g_l, ln_b_l, *, alpha, lam_init, n_tiles, tm,
                      tiles_per_seq, nb):
    r, d = xs.shape
    heads = attn.shape[1] // DA_VD
    bid = lambda i: (jnp.minimum(i // tiles_per_seq, nb), 0, 0)
    return pl.pallas_call(
        functools.partial(_odd_readout_kernel, alpha=alpha, lam_init=lam_init, heads=heads),
        grid=(n_tiles,),
        in_specs=[
            pl.BlockSpec((tm, d), lambda i: (i, 0)),
            pl.BlockSpec((1, N_MOD, d), bid),
            pl.BlockSpec((tm, attn.shape[1]), lambda i: (i, 0)),
            _resident((1, DA_VD)),
            _resident(w_out.shape),
            _resident((3, d)),
            _resident((3, d)),
        ],
        out_specs=pl.BlockSpec((tm, d), lambda i: (i, 0)),
        out_shape=jax.ShapeDtypeStruct((n_tiles * tm, d), F32),
        scratch_shapes=[pltpu.VMEM((tm, attn.shape[1]), BF16)],
        compiler_params=_cparams(("arbitrary",)),
        name="odd_readout",
    )(xs, mods_l, attn, sub_w.reshape(1, -1), w_out, ln_g_l, ln_b_l)


def _chunked_ffn_weights(w_in, w_out, th):
    d, h2 = w_in.shape
    hid = h2 // 2
    nch = hid // th
    wa = w_in[:, :hid].reshape(d, nch, th).transpose(1, 0, 2).astype(BF16)
    wb = w_in[:, hid:].reshape(d, nch, th).transpose(1, 0, 2).astype(BF16)
    wo = w_out.reshape(nch, th, d).astype(BF16)
    return wa, wb, wo


def kernel(x, c, ctx, c_ctx, w_ada, b_ada, ln_g, ln_b, w_ffn_in, w_ffn_out, w_in_even, w_out_even, hg_lb,
           hg_norm_w, pool_w, pool_scale, w_in_odd, w_out_odd, da_lambda, da_sub_w):
    nb, seq, d = x.shape
    n_ctx = ctx.shape[1]
    depth = w_ada.shape[0]
    alpha = (2 * depth) ** 0.25
    assert nb + 1 <= MOD_ROWS and seq % 512 == 0 and (nb * n_ctx) % 512 == 0

    tm = 512
    tiles_per_seq = seq // tm
    n_lat_tiles = nb * tiles_per_seq
    n_all_tiles = n_lat_tiles + nb * n_ctx // tm

    xs = jnp.concatenate([x.reshape(nb * seq, d), ctx.reshape(nb * n_ctx, d)], axis=0)
    cond = jnp.concatenate([c, c_ctx[None, :], jnp.zeros((MOD_ROWS - nb - 1, d), F32)], axis=0)
    mods = _ada_call(cond, w_ada, b_ada).reshape(depth, MOD_ROWS, N_MOD, d)

    rope = _rope_tables(seq, tm)
    gla_consts = (_gla_constants(False), _gla_constants(True))
    da_heads = w_out_odd.shape[1] // DA_VD
    common = dict(tm=tm, tiles_per_seq=tiles_per_seq, nb=nb)

    for layer in range(depth):
        last = layer == depth - 1
        m_l, g_l, b_l = mods[layer], ln_g[layer], ln_b[layer]
        wa, wb, wo = _chunked_ffn_weights(w_ffn_in[layer, 0], w_ffn_out[layer, 0], 256)
        xs = _ffn_call(xs, m_l, wa, wb, wo, g_l, b_l, mi=0, li=0, alpha=alpha, n_tiles=n_all_tiles, **common)

        n_tiles = n_lat_tiles if last else n_all_tiles
        if layer % 2 == 0:
            e = layer // 2
            z = _inproj_call(xs, m_l, w_in_even[e].astype(BF16), q_cols=hg_lb.shape[-1],
                             q_scale=HG_DK ** -0.5, rope=None, out_dtype=F32, **common)
            o_f = _gla_call(z, hg_lb[:, 0, :], gla_consts[0], e=e, d=0, rev=False, nb=nb, seq=seq, ctx=n_ctx)
            o_b = _gla_call(z, hg_lb[:, 1, :], gla_consts[1], e=e, d=1, rev=True, nb=nb, seq=seq, ctx=n_ctx)
            xs = _even_readout_call(xs, m_l, o_f, o_b, z, hg_norm_w[e], pool_w[e], pool_scale[e],
                                    w_out_even[e].astype(BF16), g_l, b_l, alpha=alpha, nb=nb, seq=seq,
                                    ctx=n_ctx)
        else:
            o = layer // 2
            lam_init = 0.8 - 0.6 * math.exp(-0.3 * layer)
            qkv = _inproj_call(xs, m_l, w_in_odd[o].astype(BF16), q_cols=da_heads * DA_VD,
                               q_scale=DA_HD ** -0.5, rope=rope, out_dtype=BF16, **common)
            attn = _attn_call(qkv, da_lambda[o], None, lam_init=lam_init, nb=nb, seq=seq, ctx=n_ctx,
                              heads=da_heads, ctx_queries=False)
            if not last:
                attn = _attn_call(qkv, da_lambda[o], attn, lam_init=lam_init, nb=nb, seq=seq, ctx=n_ctx,
                                  heads=da_heads, ctx_queries=True)
            xs = _odd_readout_call(xs, m_l, attn, da_sub_w[o], w_out_odd[o].astype(BF16), g_l, b_l,
                                   alpha=alpha, lam_init=lam_init, n_tiles=n_tiles, **common)

        wa, wb, wo = _chunked_ffn_weights(w_ffn_in[layer, 1], w_ffn_out[layer, 1], 256)
        xs = _ffn_call(xs, m_l, wa, wb, wo, g_l, b_l, mi=6, li=2, alpha=alpha, n_tiles=n_tiles, **common)

    return xs.reshape(nb, seq, d)
```
